```python
import math
import jax, jax.numpy as jnp
from jax import lax
import numpy as np

D_MODEL = 1024
BATCH = 4
SEQ = 4096
DEPTH = 1
DEC_BATCH = 32
DEC_SEQ = 1
PAST_LEN = 8192
PAGE_SIZE = 128

SSM_EXPAND = 2
D_INNER = SSM_EXPAND * D_MODEL
SSM_HEAD_DIM = 64
SSM_HEADS = D_INNER // SSM_HEAD_DIM
SSM_GROUPS = 4
HEADS_PER_GROUP = SSM_HEADS // SSM_GROUPS
D_STATE = 128
CONV_WIDTH = 4
CONV_DIM = D_INNER + 2 * SSM_GROUPS * D_STATE
SSD_CHUNK = 128
ATTN_HEAD_DIM = 64
ATTN_HEADS = D_MODEL // (2 * ATTN_HEAD_DIM)
ATTN_V_DIM = 2 * ATTN_HEAD_DIM
ATTN_WIDTH = ATTN_HEADS * ATTN_V_DIM
QK_WIDTH = ATTN_HEADS * 2 * ATTN_HEAD_DIM
ROT_DIM = ATTN_HEAD_DIM // 4
ROPE_THETA = 500000.0
Q_BLOCK = 128
N_EXPERTS = 32
TOP_K = 4
EXPERT_FF = D_MODEL
SWIGLU_LIMIT = 7.0
SWIGLU_ALPHA = 1.702
EPS = 1e-6
IN_SPLIT_SIZES = (D_INNER, CONV_DIM, SSM_HEADS, QK_WIDTH, QK_WIDTH, ATTN_WIDTH, D_MODEL, D_MODEL)
D_IN_PROJ = D_INNER + CONV_DIM + SSM_HEADS + 2 * QK_WIDTH + ATTN_WIDTH + 2 * D_MODEL

kernel_name = "hybrid_ssd_diffattn_moe_step"

F32 = jnp.float32


def _split_points():
    pts, acc = [], 0
    for s in IN_SPLIT_SIZES[:-1]:
        acc += s
        pts.append(acc)
    return pts


def _lambda_init(layer_idx):
    return 0.8 - 0.6 * math.exp(-0.3 * layer_idx)


def rmsnorm(x, w):
    xf = x.astype(F32)
    y = xf * lax.rsqrt(jnp.mean(xf * xf, axis=-1, keepdims=True) + EPS) * w.astype(F32)
    return y.astype(x.dtype)


def partial_rope(x, pos):
    half = ROT_DIM // 2
    inv_freq = ROPE_THETA ** (-jnp.arange(half, dtype=F32) * 2.0 / ROT_DIM)
    ang = pos.astype(F32)[:, None] * inv_freq[None, :]
    cos = jnp.cos(ang)[:, None, None, :]
    sin = jnp.sin(ang)[:, None, None, :]
    xr = x[..., :ROT_DIM].astype(F32)
    x1, x2 = xr[..., :half], xr[..., half:]
    rot = jnp.concatenate([x1 * cos - x2 * sin, x2 * cos + x1 * sin], axis=-1).astype(x.dtype)
    return jnp.concatenate([rot, x[..., ROT_DIM:]], axis=-1)


def _diff_weights(sc, mask, lam):
    p = jax.nn.softmax(jnp.where(mask, sc, -jnp.inf), axis=-1)
    return p[..., 0, :, :] - lam * p[..., 1, :, :]


def _diff_attn_prompt(q, k, v, lam):
    b, s = q.shape[0], q.shape[1]
    nb = s // Q_BLOCK
    scale = ATTN_HEAD_DIM ** -0.5
    qb = q.reshape(b, nb, Q_BLOCK, ATTN_HEADS, 2, ATTN_HEAD_DIM).transpose(1, 0, 2, 3, 4, 5)
    kpos = jnp.arange(s)

    def block(args):
        qi, i = args
        sc = jnp.einsum('bqhcd,bkhcd->bhcqk', qi, k).astype(F32) * scale
        qpos = i * Q_BLOCK + jnp.arange(Q_BLOCK)
        mask = kpos[None, :] <= qpos[:, None]
        a = _diff_weights(sc, mask, lam)
        return jnp.einsum('bhqk,bkhv->bqhv', a.astype(v.dtype), v)

    out = lax.map(block, (qb, jnp.arange(nb)))
    return out.transpose(1, 0, 2, 3, 4).reshape(b, s, ATTN_HEADS, ATTN_V_DIM)


def _diff_attn_decode(q, k_new, v_new, k_past, v_past, lam):
    t = q.shape[1]
    p_len = k_past.shape[1]
    scale = ATTN_HEAD_DIM ** -0.5
    s_past = jnp.einsum('bqhcd,bkhcd->bhcqk', q, k_past)
    s_new = jnp.einsum('bqhcd,bkhcd->bhcqk', q, k_new)
    sc = jnp.concatenate([s_past, s_new], axis=-1).astype(F32) * scale
    mask = jnp.concatenate([jnp.ones((t, p_len), bool), jnp.tril(jnp.ones((t, t), bool))], axis=-1)
    a = _diff_weights(sc, mask, lam).astype(v_new.dtype)
    return (jnp.einsum('bhqk,bkhv->bqhv', a[..., :p_len], v_past)
            + jnp.einsum('bhqk,bkhv->bqhv', a[..., p_len:], v_new))


def _ssd_chunked(x, dt, a, bm, cm, h0):
    b, s = x.shape[0], x.shape[1]
    nc, L = s // SSD_CHUNK, SSD_CHUNK
    x = x.reshape(b, nc, L, SSM_GROUPS, HEADS_PER_GROUP, SSM_HEAD_DIM)
    dt = dt.reshape(b, nc, L, SSM_GROUPS, HEADS_PER_GROUP)
    bm = bm.reshape(b, nc, L, SSM_GROUPS, D_STATE)
    cm = cm.reshape(b, nc, L, SSM_GROUPS, D_STATE)
    cs = jnp.cumsum(dt * a, axis=2)
    diff = cs[:, :, :, None] - cs[:, :, None, :]
    tril = jnp.tril(jnp.ones((L, L), bool))[:, :, None, None]
    decay = jnp.where(tril, jnp.exp(jnp.where(tril, diff, 0.0)), 0.0)
    cb = jnp.einsum('bclgn,bcsgn->bclsg', cm, bm)
    w = cb[..., None] * decay * dt[:, :, None]
    y_diag = jnp.einsum('bclsgh,bcsghp->bclghp', w, x)
    decay_s = jnp.exp(cs[:, :, -1:] - cs)
    states = jnp.einsum('bcsgn,bcsghp->bcghpn', bm, (decay_s * dt)[..., None] * x)
    chunk_decay = jnp.exp(cs[:, :, -1])

    def step(h, inp):
        st, dec = inp
        return h * dec[..., None, None] + st, h

    h_fin, prev = lax.scan(step, h0, (states.transpose(1, 0, 2, 3, 4, 5), chunk_decay.transpose(1, 0, 2, 3)))
    prev = prev.transpose(1, 0, 2, 3, 4, 5)
    y_off = jnp.einsum('bclgn,bcghpn->bclghp', cm, prev) * jnp.exp(cs)[..., None]
    y = (y_diag + y_off).reshape(b, s, SSM_GROUPS, HEADS_PER_GROUP, SSM_HEAD_DIM)
    return y, h_fin


def _ssd_recurrent(x, dt, a, bm, cm, h0):
    def step(h, inp):
        xt, dtt, bt, ct = inp
        h = h * jnp.exp(dtt * a)[..., None, None] + (dtt[..., None] * xt)[..., None] * bt[:, :, None, None, :]
        return h, jnp.einsum('bghpn,bgn->bghp', h, ct)

    mv = lambda z: jnp.moveaxis(z, 1, 0)
    h_fin, ys = lax.scan(step, h0, (mv(x), mv(dt), mv(bm), mv(cm)))
    return jnp.moveaxis(ys, 0, 1), h_fin


def _ssm_branch(z, xbc, dt_raw, conv_buf, h0, p, prompt):
    b, t = z.shape[0], z.shape[1]
    full = jnp.concatenate([conv_buf.astype(xbc.dtype), xbc], axis=1)
    conv = lax.conv_general_dilated(full, p['conv_w'][:, None, :].astype(full.dtype), (1,), 'VALID',
                                    dimension_numbers=('NWC', 'WIO', 'NWC'), feature_group_count=CONV_DIM)
    xbc_c = jax.nn.silu(conv + p['conv_b'])
    new_buf = full[:, t:, :]
    xs, bm, cm = jnp.split(xbc_c.astype(F32), [D_INNER, D_INNER + SSM_GROUPS * D_STATE], axis=-1)
    xs = xs.reshape(b, t, SSM_GROUPS, HEADS_PER_GROUP, SSM_HEAD_DIM)
    bm = bm.reshape(b, t, SSM_GROUPS, D_STATE)
    cm = cm.reshape(b, t, SSM_GROUPS, D_STATE)
    dt = jax.nn.softplus(dt_raw.astype(F32) + p['dt_bias'].astype(F32)).reshape(b, t, SSM_GROUPS, HEADS_PER_GROUP)
    a = -jnp.exp(p['a_log'].astype(F32)).reshape(SSM_GROUPS, HEADS_PER_GROUP)
    if prompt:
        y, h = _ssd_chunked(xs, dt, a, bm, cm, h0)
    else:
        y, h = _ssd_recurrent(xs, dt, a, bm, cm, h0)
    y = y + p['d_skip'].astype(F32).reshape(SSM_GROUPS, HEADS_PER_GROUP)[..., None] * xs
    y = y.reshape(b, t, D_INNER) * jax.nn.silu(z.astype(F32))
    yg = y.reshape(b, t, SSM_GROUPS, D_INNER // SSM_GROUPS)
    yg = yg * lax.rsqrt(jnp.mean(yg * yg, axis=-1, keepdims=True) + EPS)
    y = yg.reshape(b, t, D_INNER) * p['ssm_norm_w'].astype(F32)
    return y.astype(z.dtype), h, new_buf


def _moe(h, p):
    shp = h.shape
    t = h.reshape(-1, D_MODEL)
    logits = (t @ p['w_router'] + p['b_router']).astype(F32)
    vals, idx = lax.top_k(logits, TOP_K)
    wts = jax.nn.softmax(vals, axis=-1)
    gates = jnp.sum(jax.nn.one_hot(idx, N_EXPERTS, dtype=F32) * wts[..., None], axis=1)
    out = jnp.zeros(t.shape, F32)
    for e in range(N_EXPERTS):
        u = t @ p['w_e1'][e] + p['b_e1'][e]
        glu = jnp.minimum(u[:, ::2], SWIGLU_LIMIT)
        lin = jnp.clip(u[:, 1::2], -SWIGLU_LIMIT, SWIGLU_LIMIT)
        act = glu * jax.nn.sigmoid(SWIGLU_ALPHA * glu) * (lin + 1.0)
        out = out + gates[:, e:e + 1] * (act @ p['w_e2'][e] + p['b_e2'][e]).astype(F32)
    return out.astype(h.dtype).reshape(shp)


def _layer(x, pos, conv_buf, h0, k_past, v_past, p, lam_init, prompt):
    b, t = x.shape[0], x.shape[1]
    h = rmsnorm(x, p['norm1_w'])
    z, xbc, dt_raw, q, k, v, g_ssm, g_attn = jnp.split(h @ p['w_in'], _split_points(), axis=-1)
    y_ssm, h_new, buf_new = _ssm_branch(z, xbc, dt_raw, conv_buf, h0, p, prompt)
    q = partial_rope(rmsnorm(q.reshape(b, t, ATTN_HEADS, 2, ATTN_HEAD_DIM), p['q_norm_w']), pos)
    k = partial_rope(rmsnorm(k.reshape(b, t, ATTN_HEADS, 2, ATTN_HEAD_DIM), p['k_norm_w']), pos)
    v = v.reshape(b, t, ATTN_HEADS, ATTN_V_DIM)
    lam = (jnp.exp(jnp.sum(p['lambda_q1'].astype(F32) * p['lambda_k1'].astype(F32)))
           - jnp.exp(jnp.sum(p['lambda_q2'].astype(F32) * p['lambda_k2'].astype(F32))) + lam_init)
    if prompt:
        o = _diff_attn_prompt(q, k, v, lam)
    else:
        o = _diff_attn_decode(q, k, v, k_past, v_past, lam)
    o = (rmsnorm(o, p['attn_subln_w']) * (1.0 - lam_init)).reshape(b, t, ATTN_WIDTH)
    mix = jax.nn.sigmoid(g_ssm) * (y_ssm @ p['w_br_ssm']) + jax.nn.sigmoid(g_attn) * (o @ p['w_br_attn'])
    x = x + mix @ p['w_o']
    x = x + _moe(rmsnorm(x, p['norm2_w']), p)
    k_rows = k.reshape(b, t, ATTN_HEADS, 2 * ATTN_HEAD_DIM)
    return x, k_rows, v, h_new, buf_new


def setup_inputs(seed: int = 0) -> dict:
    key = jax.random.key(seed)
    ks = jax.random.split(key, 32)
    n_pages = PAST_LEN // PAGE_SIZE
    n_phys = (5 * DEC_BATCH * n_pages + 3) // 4
    nrm = lambda k, shape, scale: scale * jax.random.normal(k, shape, F32)
    u = jax.random.uniform(ks[10], (DEPTH, SSM_HEADS), F32)
    dt0 = jnp.exp(u * (math.log(0.1) - math.log(0.001)) + math.log(0.001))
    perm = jax.random.permutation(ks[6], n_phys)[: DEC_BATCH * n_pages]
    return {
        'x_prompt': nrm(ks[0], (BATCH, SEQ, D_MODEL), 1.0),
        'x_sample': nrm(ks[1], (DEC_BATCH, DEC_SEQ, D_MODEL), 1.0),
        'cache_k': nrm(ks[2], (DEPTH, n_phys, PAGE_SIZE, ATTN_HEADS, 2 * ATTN_HEAD_DIM), 1.0),
        'cache_v': nrm(ks[3], (DEPTH, n_phys, PAGE_SIZE, ATTN_HEADS, ATTN_V_DIM), 1.0),
        'state_ssm': nrm(ks[4], (DEPTH, DEC_BATCH, SSM_HEADS, SSM_HEAD_DIM, D_STATE), 0.5),
        'state_conv': nrm(ks[5], (DEPTH, DEC_BATCH, CONV_WIDTH - 1, CONV_DIM), 1.0),
        'page_table': perm.reshape(DEC_BATCH, n_pages).astype(jnp.int32),
        'norm1_w': 1.0 + nrm(ks[7], (DEPTH, D_MODEL), 0.02),
        'w_in': nrm(ks[8], (DEPTH, D_MODEL, D_IN_PROJ), D_MODEL ** -0.5),
        'conv_w': nrm(ks[9], (DEPTH, CONV_WIDTH, CONV_DIM), CONV_WIDTH ** -0.5),
        'conv_b': nrm(ks[11], (DEPTH, CONV_DIM), 0.02),
        'dt_bias': dt0 + jnp.log(-jnp.expm1(-dt0)),
        'a_log': jnp.log(jax.random.uniform(ks[12], (DEPTH, SSM_HEADS), F32, 1.0, 16.0)),
        'd_skip': 1.0 + nrm(ks[13], (DEPTH, SSM_HEADS), 0.1),
        'ssm_norm_w': 1.0 + nrm(ks[14], (DEPTH, D_INNER), 0.02),
        'q_norm_w': 1.0 + nrm(ks[15], (DEPTH, ATTN_HEAD_DIM), 0.02),
        'k_norm_w': 1.0 + nrm(ks[16], (DEPTH, ATTN_HEAD_DIM), 0.02),
        'lambda_q1': nrm(ks[17], (DEPTH, ATTN_HEAD_DIM), 0.1),
        'lambda_k1': nrm(ks[18], (DEPTH, ATTN_HEAD_DIM), 0.1),
        'lambda_q2': nrm(ks[19], (DEPTH, ATTN_HEAD_DIM), 0.1),
        'lambda_k2': nrm(ks[20], (DEPTH, ATTN_HEAD_DIM), 0.1),
        'attn_subln_w': 1.0 + nrm(ks[21], (DEPTH, ATTN_V_DIM), 0.02),
        'w_br_ssm': nrm(ks[22], (DEPTH, D_INNER, D_MODEL), D_INNER ** -0.5),
        'w_br_attn': nrm(ks[23], (DEPTH, ATTN_WIDTH, D_MODEL), ATTN_WIDTH ** -0.5),
        'w_o': nrm(ks[24], (DEPTH, D_MODEL, D_MODEL), D_MODEL ** -0.5),
        'norm2_w': 1.0 + nrm(ks[25], (DEPTH, D_MODEL), 0.02),
        'w_router': nrm(ks[26], (DEPTH, D_MODEL, N_EXPERTS), D_MODEL ** -0.5),
        'b_router': nrm(ks[27], (DEPTH, N_EXPERTS), 0.01),
        'w_e1': nrm(ks[28], (DEPTH, N_EXPERTS, D_MODEL, 2 * EXPERT_FF), D_MODEL ** -0.5),
        'b_e1': nrm(ks[29], (DEPTH, N_EXPERTS, 2 * EXPERT_FF), 0.02),
        'w_e2': nrm(ks[30], (DEPTH, N_EXPERTS, EXPERT_FF, D_MODEL), EXPERT_FF ** -0.5),
        'b_e2': nrm(ks[31], (DEPTH, N_EXPERTS, D_MODEL), 0.02),
    }


def reference(x_prompt, x_sample, cache_k, cache_v, state_ssm, state_conv, page_table,
              norm1_w, w_in, conv_w, conv_b, dt_bias, a_log, d_skip, ssm_norm_w,
              q_norm_w, k_norm_w, lambda_q1, lambda_k1, lambda_q2, lambda_k2, attn_subln_w,
              w_br_ssm, w_br_attn, w_o, norm2_w, w_router, b_router, w_e1, b_e1, w_e2, b_e2):
    bp, s = x_prompt.shape[0], x_prompt.shape[1]
    db, ts = x_sample.shape[0], x_sample.shape[1]
    pos_prompt = jnp.arange(s)
    pos_sample = PAST_LEN + jnp.arange(ts)
    kp_l, vp_l, hp_l, cp_l, ksm_l, vsm_l, hs_l, cs_l = [], [], [], [], [], [], [], []
    xp, xs = x_prompt, x_sample
    for l in range(DEPTH):
        p = {
            'norm1_w': norm1_w[l], 'w_in': w_in[l], 'conv_w': conv_w[l], 'conv_b': conv_b[l],
            'dt_bias': dt_bias[l], 'a_log': a_log[l], 'd_skip': d_skip[l], 'ssm_norm_w': ssm_norm_w[l],
            'q_norm_w': q_norm_w[l], 'k_norm_w': k_norm_w[l], 'lambda_q1': lambda_q1[l],
            'lambda_k1': lambda_k1[l], 'lambda_q2': lambda_q2[l], 'lambda_k2': lambda_k2[l],
            'attn_subln_w': attn_subln_w[l], 'w_br_ssm': w_br_ssm[l], 'w_br_attn': w_br_attn[l],
            'w_o': w_o[l], 'norm2_w': norm2_w[l], 'w_router': w_router[l], 'b_router': b_router[l],
            'w_e1': w_e1[l], 'b_e1': b_e1[l], 'w_e2': w_e2[l], 'b_e2': b_e2[l],
        }
        lam_init = _lambda_init(l)
        conv0 = jnp.zeros((bp, CONV_WIDTH - 1, CONV_DIM), xp.dtype)
        h0 = jnp.zeros((bp, SSM_GROUPS, HEADS_PER_GROUP, SSM_HEAD_DIM, D_STATE), F32)
        xp, kr, vr, hn, cb = _layer(xp, pos_prompt, conv0, h0, None, None, p, lam_init, True)
        kp_l.append(kr)
        vp_l.append(vr)
        hp_l.append(hn.reshape(bp, SSM_HEADS, SSM_HEAD_DIM, D_STATE).astype(x_prompt.dtype))
        cp_l.append(cb)
        k_past = cache_k[l, page_table].reshape(db, -1, ATTN_HEADS, 2, ATTN_HEAD_DIM)
        v_past = cache_v[l, page_table].reshape(db, -1, ATTN_HEADS, ATTN_V_DIM)
        hs0 = state_ssm[l].astype(F32).reshape(db, SSM_GROUPS, HEADS_PER_GROUP, SSM_HEAD_DIM, D_STATE)
        xs, kr, vr, hn, cb = _layer(xs, pos_sample, state_conv[l], hs0, k_past, v_past, p, lam_init, False)
        ksm_l.append(kr)
        vsm_l.append(vr)
        hs_l.append(hn.reshape(db, SSM_HEADS, SSM_HEAD_DIM, D_STATE).astype(state_ssm.dtype))
        cs_l.append(cb.astype(state_conv.dtype))
    return (xp, xs, jnp.stack(kp_l), jnp.stack(vp_l), jnp.stack(hp_l), jnp.stack(cp_l),
            jnp.stack(ksm_l), jnp.stack(vsm_l), jnp.stack(hs_l), jnp.stack(cs_l))
```

```python
import functools
import math

import jax
import jax.numpy as jnp
from jax import lax
from jax.experimental import pallas as pl
from jax.experimental.pallas import tpu as pltpu

F32 = jnp.float32
BF16 = jnp.bfloat16
I32 = jnp.int32
HIGHEST = lax.Precision.HIGHEST

D_MODEL = 1024
D_INNER = 2048
SSM_HEAD_DIM = 64
SSM_HEADS = 32
SSM_GROUPS = 4
GROUP_WIDTH = D_INNER // SSM_GROUPS
D_STATE = 128
CONV_WIDTH = 4
BC_WIDTH = SSM_GROUPS * D_STATE
CONV_DIM = D_INNER + 2 * BC_WIDTH
SSD_CHUNK = 128
ATTN_HEAD_DIM = 64
ATTN_HEADS = 8
ATTN_V_DIM = 128
QK_WIDTH = 1024
ROT_DIM = 16
ROPE_THETA = 500000.0
N_EXPERTS = 32
TOP_K = 4
EXPERT_FF = 1024
SWIGLU_LIMIT = 7.0
SWIGLU_ALPHA = 1.702
EPS = 1e-6
LAMBDA_INIT = 0.8 - 0.6 * math.exp(-0.3 * 0)
PAGE_SIZE = 128

LANES = 128
SUBLANES = 8
VMEM_LIMIT_BYTES = 56 * 1024 * 1024

COL_Z = 0
COL_XBC = 2048
COL_Q = 5120
COL_K = 6144
COL_V = 7168
COL_G = 8192
COL_DT = 10240
PROJ_WIDTH = 10368

NEG_BIG = -1e30


def _cparams(sem):
    return pltpu.CompilerParams(dimension_semantics=sem, vmem_limit_bytes=VMEM_LIMIT_BYTES)


def _const_spec(shape):
    nd = len(shape)
    return pl.BlockSpec(shape, lambda *_: (0,) * nd, pipeline_mode=pl.Buffered(1))


def _rms(x, w):
    return x * lax.rsqrt(jnp.mean(x * x, axis=-1, keepdims=True) + EPS) * w


def _softplus(x):
    return jnp.maximum(x, 0.0) + jnp.log1p(jnp.exp(-jnp.abs(x)))


def _silu(x):
    return x * jax.nn.sigmoid(x)


def _dot(a, b, prec=None):
    return jnp.dot(a, b, preferred_element_type=F32, precision=prec)


def _dot_nt(a, b, prec=None):
    return lax.dot_general(a, b, (((1,), (1,)), ((), ())), preferred_element_type=F32, precision=prec)


def _qk_norm_rope(t, nw, gmean, cos, sin_lo, sin_hi, prec):
    outs = []
    for hd in range(ATTN_HEADS):
        th = t[:, hd * LANES:(hd + 1) * LANES]
        ms = _dot((th * th).astype(gmean.dtype), gmean, prec)
        tn = th * lax.rsqrt(ms + EPS) * nw
        r = tn * cos + pltpu.roll(tn, LANES - ROT_DIM // 2, 1) * sin_lo + pltpu.roll(tn, ROT_DIM // 2, 1) * sin_hi
        outs.append(r)
    return jnp.concatenate(outs, axis=1)


def _rope_tables(pos):
    half = ROT_DIM // 2
    inv_freq = ROPE_THETA ** (-jnp.arange(half, dtype=F32) * 2.0 / ROT_DIM)
    ang = pos.astype(F32)[:, None] * inv_freq[None, :]
    c, s = jnp.cos(ang), jnp.sin(ang)
    n = pos.shape[0]
    rest = ATTN_HEAD_DIM - ROT_DIM
    cos64 = jnp.concatenate([c, c, jnp.ones((n, rest), F32)], axis=1)
    lo64 = jnp.concatenate([-s, jnp.zeros((n, half + rest), F32)], axis=1)
    hi64 = jnp.concatenate([jnp.zeros((n, half), F32), s, jnp.zeros((n, rest), F32)], axis=1)
    tile = lambda a: jnp.concatenate([a, a], axis=1)
    return tile(cos64), tile(lo64), tile(hi64)


def _group_mean_matrix(dtype):
    r = jnp.arange(LANES)
    return jnp.where((r[:, None] // ATTN_HEAD_DIM) == (r[None, :] // ATTN_HEAD_DIM), 1.0 / ATTN_HEAD_DIM, 0.0).astype(dtype)


def _inproj_kernel(x_ref, n1_ref, w_ref, qn_ref, kn_ref, gm_ref, cos_ref, slo_ref, shi_ref,
                   z_ref, xbc_ref, q_ref, kf_ref, kb_ref, vf_ref, vb_ref, g_ref, dt_ref):
    hb = _rms(x_ref[...], n1_ref[...]).astype(BF16)
    mm = lambda lo, hi: _dot(hb, w_ref[:, lo:hi])
    z_ref[...] = mm(COL_Z, COL_XBC).astype(BF16)
    xbc_ref[...] = mm(COL_XBC, COL_Q)
    gm, cos, slo, shi = gm_ref[...], cos_ref[...], slo_ref[...], shi_ref[...]
    q = _qk_norm_rope(mm(COL_Q, COL_K), qn_ref[...], gm, cos, slo, shi, None)
    q_ref[...] = (q * (ATTN_HEAD_DIM ** -0.5)).astype(BF16)
    k = _qk_norm_rope(mm(COL_K, COL_V), kn_ref[...], gm, cos, slo, shi, None)
    kf_ref[...] = k
    kb_ref[...] = k.astype(BF16)
    v = mm(COL_V, COL_G)
    vf_ref[...] = v
    vb_ref[...] = v.astype(BF16)
    g_ref[...] = mm(COL_G, COL_DT).astype(BF16)
    dt_ref[...] = mm(COL_DT, PROJ_WIDTH)


def _inproj_prompt(x2d, n1, w_bf16, qn, kn, tables, seq, tm):
    t = x2d.shape[0]
    nseq = seq // tm
    cos, slo, shi = tables
    row = lambda w: pl.BlockSpec((tm, w), lambda i: (i, 0))
    tab = pl.BlockSpec((tm, LANES), lambda i: (i % nseq, 0))
    out_shapes = [
        jax.ShapeDtypeStruct((t, D_INNER), BF16),
        jax.ShapeDtypeStruct((t, CONV_DIM), F32),
        jax.ShapeDtypeStruct((t, QK_WIDTH), BF16),
        jax.ShapeDtypeStruct((t, QK_WIDTH), F32),
        jax.ShapeDtypeStruct((t, QK_WIDTH), BF16),
        jax.ShapeDtypeStruct((t, QK_WIDTH), F32),
        jax.ShapeDtypeStruct((t, QK_WIDTH), BF16),
        jax.ShapeDtypeStruct((t, 2 * D_MODEL), BF16),
        jax.ShapeDtypeStruct((t, LANES), F32),
    ]
    return pl.pallas_call(
        _inproj_kernel,
        grid=(t // tm,),
        in_specs=[row(D_MODEL), _const_spec((1, D_MODEL)), _const_spec((D_MODEL, PROJ_WIDTH)),
                  _const_spec((1, LANES)), _const_spec((1, LANES)), _const_spec((LANES, LANES)), tab, tab, tab],
        out_specs=[row(D_INNER), row(CONV_DIM), row(QK_WIDTH), row(QK_WIDTH), row(QK_WIDTH), row(QK_WIDTH),
                   row(QK_WIDTH), row(2 * D_MODEL), row(LANES)],
        out_shape=out_shapes,
        compiler_params=_cparams(("arbitrary",)),
        name="inproj_prompt",
    )(x2d, n1, w_bf16, qn, kn, _group_mean_matrix(BF16), cos, slo, shi)


def _pair_cols(mat, h0, lane_lo):
    n = mat.shape[0]
    a = jnp.broadcast_to(mat[:, h0:h0 + 1], (n, LANES))
    b = jnp.broadcast_to(mat[:, h0 + 1:h0 + 2], (n, LANES))
    return jnp.where(lane_lo, a, b)


def _ssd_kernel(xbc_ref, z_ref, dt_ref, cw_ref, cb_ref, dtb_ref, alog_ref, dsk_ref, nw_ref, expand_ref,
                y_ref, hout_ref, cout_ref, ext_s, ht_s):
    c = pl.program_id(1)
    L = SSD_CHUNK

    @pl.when(c == 0)
    def _():
        ext_s[0:SUBLANES, :] = jnp.zeros((SUBLANES, CONV_DIM), F32)
        ht_s[...] = jnp.zeros(ht_s.shape, F32)

    ext_s[SUBLANES:SUBLANES + L, :] = xbc_ref[...]
    cw = cw_ref[...]
    conv = cb_ref[...] + ext_s[SUBLANES:SUBLANES + L, :] * cw[3:4]
    for j in range(1, CONV_WIDTH):
        conv = conv + ext_s[SUBLANES - j:SUBLANES - j + L, :] * cw[3 - j:4 - j]
    act = _silu(conv)
    cout_ref[0] = ext_s[L + SUBLANES - (CONV_WIDTH - 1):L + SUBLANES, :]
    ext_s[0:SUBLANES, :] = ext_s[L:L + SUBLANES, :]

    xs = act[:, :D_INNER]
    dt = _softplus(dt_ref[...] + dtb_ref[...])
    a_neg = -jnp.exp(alog_ref[...])
    rows = lax.broadcasted_iota(I32, (L, L), 0)
    cols = lax.broadcasted_iota(I32, (L, L), 1)
    tril = rows >= cols
    cs = _dot(tril.astype(F32), dt * a_neg, HIGHEST)
    cs_t = cs.T
    ecs = jnp.exp(cs)
    last = cs[L - 1:L, :]
    dec_s = jnp.exp(last - cs)
    cdec = _dot(jnp.broadcast_to(jnp.exp(last), (SUBLANES, LANES)), expand_ref[...], HIGHEST)[0:1, :]
    lane_lo = lax.broadcasted_iota(I32, (L, LANES), 1) < SSM_HEAD_DIM

    y_parts = []
    for g in range(SSM_GROUPS):
        bm = act[:, D_INNER + g * D_STATE:D_INNER + (g + 1) * D_STATE]
        cm = act[:, D_INNER + BC_WIDTH + g * D_STATE:D_INNER + BC_WIDTH + (g + 1) * D_STATE]
        bm_b, cm_b = bm.astype(BF16), cm.astype(BF16)
        cbm = jnp.where(tril, _dot_nt(cm_b, bm_b), 0.0)
        ht_g = ht_s[g]
        y_off_g = _dot(cm_b, ht_g.astype(BF16))
        xw_parts = []
        for pr in range(GROUP_WIDTH // LANES):
            h0 = g * (GROUP_WIDTH // SSM_HEAD_DIM) + 2 * pr
            lo = g * GROUP_WIDTH + pr * LANES
            xdt = xs[:, lo:lo + LANES] * _pair_cols(dt, h0, lane_lo)
            y_pair = y_off_g[:, pr * LANES:(pr + 1) * LANES] * _pair_cols(ecs, h0, lane_lo)
            for hh in range(2):
                h = h0 + hh
                diff = jnp.broadcast_to(cs[:, h:h + 1], (L, L)) - jnp.broadcast_to(cs_t[h:h + 1, :], (L, L))
                w = (cbm * jnp.exp(jnp.minimum(diff, 0.0))).astype(BF16)
                keep = lane_lo if hh == 0 else jnp.logical_not(lane_lo)
                y_pair = y_pair + _dot(w, jnp.where(keep, xdt, 0.0).astype(BF16))
            y_parts.append(y_pair)
            xw_parts.append((xdt * _pair_cols(dec_s, h0, lane_lo)).astype(BF16))
        st_t = _dot(bm.T.astype(BF16), jnp.concatenate(xw_parts, axis=1))
        ht_s[g] = ht_g * cdec[:, g * GROUP_WIDTH:(g + 1) * GROUP_WIDTH] + st_t

    y = jnp.concatenate(y_parts, axis=1) + dsk_ref[...] * xs
    y = y * _silu(z_ref[...].astype(F32))
    nw = nw_ref[...]
    outs = []
    for g in range(SSM_GROUPS):
        yg = y[:, g * GROUP_WIDTH:(g + 1) * GROUP_WIDTH]
        outs.append(yg * lax.rsqrt(jnp.mean(yg * yg, axis=-1, keepdims=True) + EPS) * nw[:, g * GROUP_WIDTH:(g + 1) * GROUP_WIDTH])
    y_ref[...] = jnp.concatenate(outs, axis=1).astype(y_ref.dtype)

    @pl.when(c == pl.num_programs(1) - 1)
    def _():
        for g in range(SSM_GROUPS):
            hout_ref[0, g * GROUP_WIDTH:(g + 1) * GROUP_WIDTH, :] = ht_s[g].T


def _head_expand_matrix():
    r = jnp.arange(LANES)[:, None]
    c = jnp.arange(D_INNER)[None, :] // SSM_HEAD_DIM
    return (r == c).astype(F32)


def _pad_lanes(v, fill=0.0):
    return jnp.pad(v.astype(F32), (0, LANES - v.shape[0]), constant_values=fill)[None, :]


def _ssd_prompt(xbc, z, dt, p, batch, seq):
    nc = seq // SSD_CHUNK
    blk = lambda w: pl.BlockSpec((SSD_CHUNK, w), lambda b, c: (b * nc + c, 0))
    return pl.pallas_call(
        _ssd_kernel,
        grid=(batch, nc),
        in_specs=[blk(CONV_DIM), blk(D_INNER), blk(LANES),
                  _const_spec((CONV_WIDTH, CONV_DIM)), _const_spec((1, CONV_DIM)), _const_spec((1, LANES)),
                  _const_spec((1, LANES)), _const_spec((1, D_INNER)), _const_spec((1, D_INNER)),
                  _const_spec((LANES, D_INNER))],
        out_specs=[blk(D_INNER),
                   pl.BlockSpec((1, D_INNER, D_STATE), lambda b, c: (b, 0, 0)),
                   pl.BlockSpec((1, CONV_WIDTH - 1, CONV_DIM), lambda b, c: (b, 0, 0))],
        out_shape=[jax.ShapeDtypeStruct((batch * seq, D_INNER), BF16),
                   jax.ShapeDtypeStruct((batch, D_INNER, D_STATE), F32),
                   jax.ShapeDtypeStruct((batch, CONV_WIDTH - 1, CONV_DIM), F32)],
        scratch_shapes=[pltpu.VMEM((SSD_CHUNK + 2 * SUBLANES, CONV_DIM), F32),
                        pltpu.VMEM((SSM_GROUPS, D_STATE, GROUP_WIDTH), F32)],
        compiler_params=_cparams(("arbitrary", "arbitrary")),
        name="ssd_prompt",
    )(xbc, z, dt, p['conv_w'], p['conv_b'], p['dt_bias'], p['a_log'], p['d_skip'], p['ssm_norm_w'],
      _head_expand_matrix())


def _lambda_value(lq1, lk1, lq2, lk2):
    s1 = jnp.sum(lq1 * lk1, axis=-1, keepdims=True)
    s2 = jnp.sum(lq2 * lk2, axis=-1, keepdims=True)
    return jnp.exp(s1) - jnp.exp(s2) + LAMBDA_INIT


def _flash_kernel(q_ref, k_ref, v_ref, lq1_ref, lk1_ref, lq2_ref, lk2_ref, sub_ref, o_ref,
                  m_s, l_s, acc_s, *, tq):
    i = pl.program_id(2)
    q = q_ref[...]
    lane_lo = lax.broadcasted_iota(I32, (tq, LANES), 1) < ATTN_HEAD_DIM
    zero = jnp.zeros_like(q)
    qc = (jnp.where(lane_lo, q, zero), jnp.where(lane_lo, zero, q))
    m_s[...] = jnp.full(m_s.shape, NEG_BIG, F32)
    l_s[...] = jnp.zeros(l_s.shape, F32)
    acc_s[...] = jnp.zeros(acc_s.shape, F32)

    def block(j, masked):
        start = pl.multiple_of(j * tq, tq)
        k = k_ref[pl.ds(start, tq), :]
        v = v_ref[pl.ds(start, tq), :]
        for c in range(2):
            s = _dot_nt(qc[c], k)
            if masked:
                r = lax.broadcasted_iota(I32, (tq, tq), 0)
                cc = lax.broadcasted_iota(I32, (tq, tq), 1)
                s = jnp.where(cc <= r, s, NEG_BIG)
            m_old = m_s[c]
            m_new = jnp.maximum(m_old, jnp.max(s, axis=-1, keepdims=True))
            alpha = jnp.exp(m_old - m_new)
            pexp = jnp.exp(s - m_new)
            l_s[c] = alpha * l_s[c] + jnp.sum(pexp, axis=-1, keepdims=True)
            acc_s[c] = alpha * acc_s[c] + _dot(pexp.astype(BF16), v)
            m_s[c] = m_new

    def body(j, carry):
        block(j, False)
        return carry

    lax.fori_loop(0, i, body, 0)
    block(i, True)

    lam = _lambda_value(lq1_ref[...], lk1_ref[...], lq2_ref[...], lk2_ref[...])
    o = acc_s[0] / l_s[0] - lam * (acc_s[1] / l_s[1])
    o = _rms(o, sub_ref[...]) * (1.0 - LAMBDA_INIT)
    o_ref[...] = o.astype(o_ref.dtype)


def _flash_prompt(q, k, v, p, batch, seq, tq):
    nq = seq // tq
    lam_spec = _const_spec((1, ATTN_HEAD_DIM))
    return pl.pallas_call(
        functools.partial(_flash_kernel, tq=tq),
        grid=(batch, ATTN_HEADS, nq),
        in_specs=[pl.BlockSpec((tq, LANES), lambda b, h, i: (b * nq + i, h)),
                  pl.BlockSpec((seq, LANES), lambda b, h, i: (b, h)),
                  pl.BlockSpec((seq, LANES), lambda b, h, i: (b, h)),
                  lam_spec, lam_spec, lam_spec, lam_spec, _const_spec((1, ATTN_V_DIM))],
        out_specs=pl.BlockSpec((tq, LANES), lambda b, h, i: (b * nq + i, h)),
        out_shape=jax.ShapeDtypeStruct((batch * seq, QK_WIDTH), BF16),
        scratch_shapes=[pltpu.VMEM((2, tq, 1), F32), pltpu.VMEM((2, tq, 1), F32), pltpu.VMEM((2, tq, LANES), F32)],
        compiler_params=_cparams(("arbitrary", "arbitrary", "arbitrary")),
        name="flash_prompt",
    )(q, k, v, p['lambda_q1'], p['lambda_k1'], p['lambda_q2'], p['lambda_k2'], p['attn_subln_w'])


def _merge_kernel(y_ref, o_ref, g_ref, x_ref, wbs_ref, wba_ref, wo_ref, x1_ref, *, prec):
    wd = wbs_ref.dtype
    a = _dot(y_ref[...].astype(wd), wbs_ref[...], prec)
    b = _dot(o_ref[...].astype(wd), wba_ref[...], prec)
    g = g_ref[...].astype(F32)
    mix = jax.nn.sigmoid(g[:, :D_MODEL]) * a + jax.nn.sigmoid(g[:, D_MODEL:]) * b
    x1_ref[...] = x_ref[...] + _dot(mix.astype(wd), wo_ref[...], prec)


def _merge(y, o, g, x2d, wbs, wba, wo, tm, prec, name):
    t = x2d.shape[0]
    row = lambda w: pl.BlockSpec((tm, w), lambda i: (i, 0))
    return pl.pallas_call(
        functools.partial(_merge_kernel, prec=prec),
        grid=(t // tm,),
        in_specs=[row(D_INNER), row(QK_WIDTH), row(2 * D_MODEL), row(D_MODEL),
                  _const_spec((D_INNER, D_MODEL)), _const_spec((QK_WIDTH, D_MODEL)), _const_spec((D_MODEL, D_MODEL))],
        out_specs=row(D_MODEL),
        out_shape=jax.ShapeDtypeStruct((t, D_MODEL), F32),
        compiler_params=_cparams(("arbitrary",)),
        name=name,
    )(y, o, g, x2d, wbs, wba, wo)


SAMPLE_PROJ_TILE = 1152


def _proj_sample_kernel(x_ref, n1_ref, w_ref, o_ref):
    o_ref[...] = _dot(_rms(x_ref[...], n1_ref[...]), w_ref[...], HIGHEST)


def _proj_sample(x2d, n1, w_f32):
    m = x2d.shape[0]
    tn = SAMPLE_PROJ_TILE
    return pl.pallas_call(
        _proj_sample_kernel,
        grid=(PROJ_WIDTH // tn,),
        in_specs=[_const_spec((m, D_MODEL)), _const_spec((1, D_MODEL)), pl.BlockSpec((D_MODEL, tn), lambda j: (0, j))],
        out_specs=pl.BlockSpec((m, tn), lambda j: (0, j)),
        out_shape=jax.ShapeDtypeStruct((m, PROJ_WIDTH), F32),
        compiler_params=_cparams(("arbitrary",)),
        name="proj_sample",
    )(x2d, n1, w_f32)


def _sample_pre_kernel(proj_ref, sconv_ref, cw_ref, cb_ref, dtb_ref, qn_ref, kn_ref, gm_ref, cos_ref, slo_ref, shi_ref,
                       q_ref, k_ref, act_ref, dt_ref, cout_ref):
    gm, cos, slo, shi = gm_ref[...], cos_ref[...], slo_ref[...], shi_ref[...]
    q = _qk_norm_rope(proj_ref[:, COL_Q:COL_K], qn_ref[...], gm, cos, slo, shi, HIGHEST)
    q_ref[...] = q * (ATTN_HEAD_DIM ** -0.5)
    k_ref[...] = _qk_norm_rope(proj_ref[:, COL_K:COL_V], kn_ref[...], gm, cos, slo, shi, HIGHEST)
    xbc = proj_ref[:, COL_XBC:COL_Q]
    cw = cw_ref[...]
    conv = cb_ref[...] + xbc * cw[CONV_WIDTH - 1:CONV_WIDTH]
    for j in range(CONV_WIDTH - 1):
        conv = conv + sconv_ref[:, j, :] * cw[j:j + 1]
    act_ref[...] = _silu(conv)
    dt_ref[...] = _softplus(proj_ref[:, COL_DT:PROJ_WIDTH] + dtb_ref[...])
    for j in range(CONV_WIDTH - 2):
        cout_ref[:, j, :] = sconv_ref[:, j + 1, :]
    cout_ref[:, CONV_WIDTH - 2, :] = xbc


def _sample_pre(proj, state_conv, p, tables):
    m = proj.shape[0]
    cos, slo, shi = tables
    whole = lambda a: _const_spec(a.shape)
    args = (proj, state_conv, p['conv_w'], p['conv_b'], p['dt_bias'], p['q_norm_w'], p['k_norm_w'],
            _group_mean_matrix(F32), cos, slo, shi)
    out_shape = [jax.ShapeDtypeStruct((m, QK_WIDTH), F32), jax.ShapeDtypeStruct((m, QK_WIDTH), F32),
                 jax.ShapeDtypeStruct((m, CONV_DIM), F32), jax.ShapeDtypeStruct((m, LANES), F32),
                 jax.ShapeDtypeStruct((m, CONV_WIDTH - 1, CONV_DIM), F32)]
    return pl.pallas_call(
        _sample_pre_kernel,
        grid=(1,),
        in_specs=[whole(a) for a in args],
        out_specs=[_const_spec(s.shape) for s in out_shape],
        out_shape=out_shape,
        compiler_params=_cparams(("arbitrary",)),
        name="sample_pre",
    )(*args)


def _transpose_rows(a):
    r = a.shape[0]
    if r < LANES:
        a = jnp.concatenate([a, jnp.zeros((LANES - r, LANES), a.dtype)], axis=0)
    return a.T


def _ssm_step_kernel(x_ref, bm_ref, cm_ref, dt_ref, alog_ref, dsk_ref, s_ref, so_ref, y_ref):
    pr = pl.program_id(0)
    nb = x_ref.shape[0]
    x = x_ref[...]
    dt = dt_ref[...]
    decay = jnp.exp(dt * (-jnp.exp(alog_ref[...])))
    lane = lax.broadcasted_iota(I32, (nb, LANES), 1)
    lane_lo = lane < SSM_HEAD_DIM
    pick = lambda mat, hh: jnp.sum(jnp.where(lane == 2 * pr + hh, mat, 0.0), axis=-1, keepdims=True)
    dt_pair = jnp.where(lane_lo, pick(dt, 0), pick(dt, 1))
    xd_t = _transpose_rows(x * dt_pair)
    dec_cols = [pick(decay, 0), pick(decay, 1)]
    bm, cm = bm_ref[...], cm_ref[...]
    lane_sq = lax.broadcasted_iota(I32, (LANES, LANES), 1)
    y_t = jnp.zeros((LANES, LANES), F32)
    for b in range(nb):
        cols = []
        for hh in range(2):
            hn = (s_ref[b, hh] * dec_cols[hh][b:b + 1, :]
                  + xd_t[hh * SSM_HEAD_DIM:(hh + 1) * SSM_HEAD_DIM, b:b + 1] * bm[b:b + 1, :])
            so_ref[b, hh] = hn
            cols.append(jnp.sum(hn * cm[b:b + 1, :], axis=-1, keepdims=True))
        y_t = jnp.where(lane_sq == b, jnp.concatenate(cols, axis=0), y_t)
    y_ref[...] = y_t.T[0:nb, :] + dsk_ref[...] * x


def _ssm_step(act, dtv, p, state):
    nb = act.shape[0]
    npairs = D_INNER // LANES
    pairs_per_group = GROUP_WIDTH // LANES
    xcol = lambda base: pl.BlockSpec((nb, LANES), lambda pr: (0, base + pr // pairs_per_group))
    st = pl.BlockSpec((nb, 2, SSM_HEAD_DIM, D_STATE), lambda pr: (0, pr, 0, 0))
    return pl.pallas_call(
        _ssm_step_kernel,
        grid=(npairs,),
        in_specs=[pl.BlockSpec((nb, LANES), lambda pr: (0, pr)), xcol(D_INNER // LANES), xcol((D_INNER + BC_WIDTH) // LANES),
                  _const_spec((nb, LANES)), _const_spec((1, LANES)), pl.BlockSpec((1, LANES), lambda pr: (0, pr)), st],
        out_specs=[st, pl.BlockSpec((nb, LANES), lambda pr: (0, pr))],
        out_shape=[jax.ShapeDtypeStruct(state.shape, F32), jax.ShapeDtypeStruct((nb, D_INNER), F32)],
        compiler_params=_cparams(("arbitrary",)),
        name="ssm_step",
    )(act, act, act, dtv, p['a_log'], p['d_skip'], state)


def _gate_norm_kernel(y_ref, z_ref, nw_ref, o_ref):
    y = y_ref[...] * _silu(z_ref[...])
    nw = nw_ref[...]
    outs = []
    for g in range(SSM_GROUPS):
        yg = y[:, g * GROUP_WIDTH:(g + 1) * GROUP_WIDTH]
        outs.append(yg * lax.rsqrt(jnp.mean(yg * yg, axis=-1, keepdims=True) + EPS) * nw[:, g * GROUP_WIDTH:(g + 1) * GROUP_WIDTH])
    o_ref[...] = jnp.concatenate(outs, axis=1)


def _gate_norm(y_pre, z, nw):
    return pl.pallas_call(
        _gate_norm_kernel,
        grid=(1,),
        in_specs=[_const_spec(y_pre.shape), _const_spec(z.shape), _const_spec(nw.shape)],
        out_specs=_const_spec(y_pre.shape),
        out_shape=jax.ShapeDtypeStruct(y_pre.shape, F32),
        compiler_params=_cparams(("arbitrary",)),
        name="gate_norm_sample",
    )(y_pre, z, nw)


DECODE_PAGES_PER_STEP = 4


def _decode_kernel(pt_ref, q_ref, kn_ref, vn_ref, lq1_ref, lk1_ref, lq2_ref, lk2_ref, sub_ref, *rest):
    pg = DECODE_PAGES_PER_STEP
    k_refs, v_refs = rest[:pg], rest[pg:2 * pg]
    o_ref, m_s, l_s, acc_s = rest[2 * pg:]
    j = pl.program_id(1)
    nrow = 2 * ATTN_HEADS
    half = ATTN_HEAD_DIM

    @pl.when(j == 0)
    def _():
        m_s[...] = jnp.full(m_s.shape, NEG_BIG, F32)
        l_s[...] = jnp.zeros(l_s.shape, F32)
        acc_s[...] = jnp.zeros(acc_s.shape, F32)

    q_t = _transpose_rows(q_ref[0])

    def score_rows(k_tiles):
        rows = []
        for h in range(ATTN_HEADS):
            qcol = jnp.broadcast_to(q_t[:, h:h + 1], (LANES, LANES))
            r1, r2 = [], []
            for kt in k_tiles[h]:
                prod = kt.T * qcol
                r1.append(jnp.sum(prod[0:half, :], axis=0, keepdims=True))
                r2.append(jnp.sum(prod[half:, :], axis=0, keepdims=True))
            rows.append(jnp.concatenate(r1, axis=1))
            rows.append(jnp.concatenate(r2, axis=1))
        return jnp.concatenate(rows, axis=0)

    def update(s, v_tiles):
        m_old = m_s[...]
        m_new = jnp.maximum(m_old, jnp.max(s, axis=-1, keepdims=True))
        alpha = jnp.exp(m_old - m_new)
        pexp = jnp.exp(s - m_new)
        l_s[...] = alpha * l_s[...] + jnp.sum(pexp, axis=-1, keepdims=True)
        m_s[...] = m_new
        for h in range(ATTN_HEADS):
            v_ts = [vt.T for vt in v_tiles[h]]
            for c in range(2):
                r = 2 * h + c
                acc = acc_s[r] * alpha[r:r + 1, :]
                for n, v_t in enumerate(v_ts):
                    acc = acc + v_t * pexp[r:r + 1, n * LANES:(n + 1) * LANES]
                acc_s[r] = acc

    k_tiles = [[k_refs[n][0, 0, :, h, :] for n in range(pg)] for h in range(ATTN_HEADS)]
    v_tiles = [[v_refs[n][0, 0, :, h, :] for n in range(pg)] for h in range(ATTN_HEADS)]
    update(score_rows(k_tiles), v_tiles)

    @pl.when(j == pl.num_programs(1) - 1)
    def _():
        row0 = lax.broadcasted_iota(I32, (LANES, LANES), 0) == 0
        kn, vn = kn_ref[0], vn_ref[0]
        k_new = [[jnp.where(row0, jnp.broadcast_to(kn[h:h + 1, :], (LANES, LANES)), 0.0)] for h in range(ATTN_HEADS)]
        v_new = [[jnp.where(row0, jnp.broadcast_to(vn[h:h + 1, :], (LANES, LANES)), 0.0)] for h in range(ATTN_HEADS)]
        s_new = score_rows(k_new)
        s_new = jnp.where(lax.broadcasted_iota(I32, s_new.shape, 1) == 0, s_new, NEG_BIG)
        update(s_new, v_new)

        lane_sq = lax.broadcasted_iota(I32, (LANES, LANES), 1)
        l_all = l_s[...]
        o_t = [jnp.zeros((LANES, LANES), F32), jnp.zeros((LANES, LANES), F32)]
        for h in range(ATTN_HEADS):
            for c in range(2):
                r = 2 * h + c
                col = jnp.sum(acc_s[r], axis=-1, keepdims=True) / l_all[r:r + 1, :]
                o_t[c] = jnp.where(lane_sq == h, col, o_t[c])
        lam = _lambda_value(lq1_ref[...], lk1_ref[...], lq2_ref[...], lk2_ref[...])
        o = o_t[0].T[0:ATTN_HEADS, :] - lam * o_t[1].T[0:ATTN_HEADS, :]
        o_ref[0] = _rms(o, sub_ref[...]) * (1.0 - LAMBDA_INIT)


def _decode_attention(q, k_new, v_new, cache_k, cache_v, page_table, p):
    db, n_pages = page_table.shape
    pg = DECODE_PAGES_PER_STEP
    nj = n_pages // pg
    per_b = pl.BlockSpec((1, ATTN_HEADS, LANES), lambda b, j, pt: (b, 0, 0))
    page = lambda n: pl.BlockSpec((1, 1, PAGE_SIZE, ATTN_HEADS, LANES),
                                  lambda b, j, pt: (0, pt[b * n_pages + j * pg + n], 0, 0, 0))
    lam_spec = pl.BlockSpec((1, ATTN_HEAD_DIM), lambda b, j, pt: (0, 0))
    nrow = 2 * ATTN_HEADS
    return pl.pallas_call(
        _decode_kernel,
        grid_spec=pltpu.PrefetchScalarGridSpec(
            num_scalar_prefetch=1,
            grid=(db, nj),
            in_specs=[per_b, per_b, per_b, lam_spec, lam_spec, lam_spec, lam_spec,
                      pl.BlockSpec((1, ATTN_V_DIM), lambda b, j, pt: (0, 0))]
                     + [page(n) for n in range(pg)] + [page(n) for n in range(pg)],
            out_specs=per_b,
            scratch_shapes=[pltpu.VMEM((nrow, 1), F32), pltpu.VMEM((nrow, 1), F32), pltpu.VMEM((nrow, LANES, LANES), F32)]),
        out_shape=jax.ShapeDtypeStruct((db, ATTN_HEADS, LANES), F32),
        compiler_params=_cparams(("arbitrary", "arbitrary")),
        name="decode_attention",
    )(page_table.reshape(-1), q, k_new, v_new, p['lambda_q1'], p['lambda_k1'], p['lambda_q2'], p['lambda_k2'],
      p['attn_subln_w'], *([cache_k] * pg), *([cache_v] * pg))


def _sample_mixer(x_sample, state_ssm, state_conv, cache_k, cache_v, page_table, norm1_w, w_in_p, p,
                  w_br_ssm, w_br_attn, w_o, past_len):
    db = x_sample.shape[0]
    x2d = x_sample.reshape(db, D_MODEL)
    proj = _proj_sample(x2d, norm1_w[None, :], w_in_p)
    tables = _rope_tables(jnp.full((1,), past_len, I32))
    q, k, act, dtv, conv_new = _sample_pre(proj, state_conv, p, tables)
    v = proj[:, COL_V:COL_G]
    state_new, y_pre = _ssm_step(act, dtv, p, state_ssm)
    y_ssm = _gate_norm(y_pre, proj[:, COL_Z:COL_XBC], p['ssm_norm_w'])
    h3 = lambda a: a.reshape(db, ATTN_HEADS, LANES)
    o = _decode_attention(h3(q), h3(k), h3(v), cache_k, cache_v, page_table, p)
    x1 = _merge(y_ssm, o.reshape(db, QK_WIDTH), proj[:, COL_G:COL_DT], x2d, w_br_ssm, w_br_attn, w_o,
                tm=db, prec=HIGHEST, name="merge_sample")
    return x1, k, v, state_new, conv_new


ROUTE_TILE = 256
EXPERT_TILE = 256
META_IDX, META_WT, META_RANK = 0, 4, 8


def _route_kernel(x1_ref, n2_ref, wr_ref, br_ref, h2_ref, meta_ref, cnt_ref, carry_s):
    @pl.when(pl.program_id(0) == 0)
    def _():
        carry_s[...] = jnp.zeros(carry_s.shape, F32)

    tr = x1_ref.shape[0]
    h2 = _rms(x1_ref[...], n2_ref[...])
    h2_ref[...] = h2
    work = _dot(h2, wr_ref[...], HIGHEST) + br_ref[...]
    lane = lax.broadcasted_iota(I32, (tr, LANES), 1)
    lane_f = lane.astype(F32)
    vals, hots, idxs = [], [], []
    for _ in range(TOP_K):
        m = jnp.max(work, axis=-1, keepdims=True)
        idx = jnp.min(jnp.where(work == m, lane_f, float(LANES)), axis=-1, keepdims=True)
        hot = lane_f == idx
        vals.append(m)
        hots.append(hot)
        idxs.append(idx)
        work = jnp.where(hot, -jnp.inf, work)
    es = [jnp.exp(v - vals[0]) for v in vals]
    denom = es[0] + es[1] + es[2] + es[3]
    sel = jnp.zeros((tr, LANES), F32)
    for hot in hots:
        sel = sel + jnp.where(hot, 1.0, 0.0)
    rows = lax.broadcasted_iota(I32, (tr, tr), 0)
    cols = lax.broadcasted_iota(I32, (tr, tr), 1)
    before = jnp.where(cols < rows, 1.0, 0.0).astype(BF16)
    rank_all = carry_s[0:1, :] + _dot(before, sel.astype(BF16))
    meta = jnp.zeros((tr, LANES), F32)
    for k in range(TOP_K):
        rank_k = jnp.sum(jnp.where(hots[k], rank_all, 0.0), axis=-1, keepdims=True)
        meta = jnp.where(lane == META_IDX + k, idxs[k], meta)
        meta = jnp.where(lane == META_WT + k, es[k] / denom, meta)
        meta = jnp.where(lane == META_RANK + k, rank_k, meta)
    meta_ref[...] = meta
    carry_s[...] = carry_s[...] + jnp.sum(sel, axis=0, keepdims=True)
    cnt_ref[...] = carry_s[...]


def _route(x1_all, n2, wr_pad, br_pad):
    tp = x1_all.shape[0]
    tr = ROUTE_TILE
    row = lambda w: pl.BlockSpec((tr, w), lambda i: (i, 0))
    return pl.pallas_call(
        _route_kernel,
        grid=(tp // tr,),
        in_specs=[row(D_MODEL), _const_spec((1, D_MODEL)), _const_spec((D_MODEL, LANES)), _const_spec((1, LANES))],
        out_specs=[row(D_MODEL), row(LANES), pl.BlockSpec((SUBLANES, LANES), lambda i: (0, 0))],
        out_shape=[jax.ShapeDtypeStruct((tp, D_MODEL), F32), jax.ShapeDtypeStruct((tp, LANES), F32),
                   jax.ShapeDtypeStruct((SUBLANES, LANES), F32)],
        scratch_shapes=[pltpu.VMEM((SUBLANES, LANES), F32)],
        compiler_params=_cparams(("arbitrary",)),
        name="moe_route",
    )(x1_all, n2, wr_pad, br_pad)


def _row_copy(src, src_row, dst, dst_row, sem):
    return pltpu.make_async_copy(src.at[pl.ds(src_row, 1), :], dst.at[pl.ds(dst_row, 1), :], sem)


def _dispatch_kernel(dest_ref, pad_ref, h2_hbm, xs_hbm, zrow_s, sem, zsem):
    i = pl.program_id(0)
    tr = ROUTE_TILE

    @pl.when(i == 0)
    def _():
        zrow_s[...] = jnp.zeros(zrow_s.shape, F32)
        for e in range(N_EXPERTS):
            lo, hi = pad_ref[0, e], pad_ref[1, e]
            lax.fori_loop(lo, hi, lambda r, c: (_row_copy(zrow_s, 0, xs_hbm, r, zsem).start(), c)[1], 0)
        for e in range(N_EXPERTS):
            lo, hi = pad_ref[0, e], pad_ref[1, e]
            lax.fori_loop(lo, hi, lambda r, c: (_row_copy(zrow_s, 0, xs_hbm, r, zsem).wait(), c)[1], 0)

    def issue(t, c):
        for k in range(TOP_K):
            _row_copy(h2_hbm, i * tr + t, xs_hbm, dest_ref[0, 0, t * TOP_K + k], sem).start()
        return c

    def drain(t, c):
        for k in range(TOP_K):
            _row_copy(h2_hbm, 0, xs_hbm, 0, sem).wait()
        return c

    lax.fori_loop(0, tr, issue, 0)
    lax.fori_loop(0, tr, drain, 0)


def _dispatch(dest3, pad_bounds, h2, n_rows):
    nt = dest3.shape[0]
    return pl.pallas_call(
        _dispatch_kernel,
        grid=(nt,),
        in_specs=[pl.BlockSpec((1, 1, ROUTE_TILE * TOP_K), lambda i: (i, 0, 0), memory_space=pltpu.SMEM),
                  pl.BlockSpec(memory_space=pltpu.SMEM),
                  pl.BlockSpec(memory_space=pl.ANY)],
        out_specs=pl.BlockSpec(memory_space=pl.ANY),
        out_shape=jax.ShapeDtypeStruct((n_rows, D_MODEL), F32),
        scratch_shapes=[pltpu.VMEM((SUBLANES, D_MODEL), F32), pltpu.SemaphoreType.DMA, pltpu.SemaphoreType.DMA],
        compiler_params=_cparams(("arbitrary",)),
        name="moe_dispatch",
    )(dest3, pad_bounds, h2)


def _experts_kernel(te_ref, nt_ref, xs_ref, w1g_ref, w1l_ref, b1g_ref, b1l_ref, w2_ref, b2_ref, ys_ref):
    @pl.when(pl.program_id(0) < nt_ref[0])
    def _():
        x = xs_ref[...].astype(BF16)
        glu = jnp.minimum(_dot(x, w1g_ref[0]) + b1g_ref[0], SWIGLU_LIMIT)
        lin = jnp.clip(_dot(x, w1l_ref[0]) + b1l_ref[0], -SWIGLU_LIMIT, SWIGLU_LIMIT)
        act = glu * jax.nn.sigmoid(SWIGLU_ALPHA * glu) * (lin + 1.0)
        ys_ref[...] = _dot(act.astype(BF16), w2_ref[0]) + b2_ref[0]


def _experts(tile_expert, n_tiles, xs, w1g, w1l, b1g, b1l, w2, b2):
    tm = EXPERT_TILE
    nt_max = xs.shape[0] // tm
    rows = pl.BlockSpec((tm, D_MODEL), lambda i, te, nt: (jnp.minimum(i, nt[0] - 1), 0))
    wspec = lambda a, b: pl.BlockSpec((1, a, b), lambda i, te, nt: (te[i], 0, 0))
    return pl.pallas_call(
        _experts_kernel,
        grid_spec=pltpu.PrefetchScalarGridSpec(
            num_scalar_prefetch=2,
            grid=(nt_max,),
            in_specs=[rows, wspec(D_MODEL, EXPERT_FF), wspec(D_MODEL, EXPERT_FF), wspec(1, EXPERT_FF), wspec(1, EXPERT_FF),
                      wspec(EXPERT_FF, D_MODEL), wspec(1, D_MODEL)],
            out_specs=rows),
        out_shape=jax.ShapeDtypeStruct(xs.shape, F32),
        compiler_params=_cparams(("arbitrary",)),
        name="moe_experts",
    )(tile_expert, n_tiles, xs, w1g, w1l, b1g, b1l, w2, b2)


def _combine_kernel(dest_ref, meta_ref, x1_ref, ys_hbm, yp_ref, ys_ref, buf_s, sem, *, n_prompt_tiles):
    i = pl.program_id(0)
    tr = ROUTE_TILE

    def issue(t, c):
        for k in range(TOP_K):
            pltpu.make_async_copy(ys_hbm.at[pl.ds(dest_ref[0, 0, t * TOP_K + k], 1), :],
                                  buf_s.at[k, pl.ds(t, 1), :], sem).start()
        return c

    def drain(t, c):
        for k in range(TOP_K):
            pltpu.make_async_copy(ys_hbm.at[pl.ds(0, 1), :], buf_s.at[k, pl.ds(0, 1), :], sem).wait()
        return c

    lax.fori_loop(0, tr, issue, 0)
    lax.fori_loop(0, tr, drain, 0)
    meta = meta_ref[...]
    moe = jnp.zeros((tr, D_MODEL), F32)
    for k in range(TOP_K):
        moe = moe + meta[:, META_WT + k:META_WT + k + 1] * buf_s[k]
    y = x1_ref[...] + moe

    @pl.when(i < n_prompt_tiles)
    def _():
        yp_ref[...] = y

    @pl.when(i >= n_prompt_tiles)
    def _():
        ys_ref[...] = y


def _combine(dest3, meta, x1_all, ys, n_prompt_tiles):
    tr = ROUTE_TILE
    nt = dest3.shape[0]
    row = lambda w: pl.BlockSpec((tr, w), lambda i: (i, 0))
    return pl.pallas_call(
        functools.partial(_combine_kernel, n_prompt_tiles=n_prompt_tiles),
        grid=(nt,),
        in_specs=[pl.BlockSpec((1, 1, tr * TOP_K), lambda i: (i, 0, 0), memory_space=pltpu.SMEM),
                  row(LANES), row(D_MODEL), pl.BlockSpec(memory_space=pl.ANY)],
        out_specs=[pl.BlockSpec((tr, D_MODEL), lambda i: (jnp.minimum(i, n_prompt_tiles - 1), 0)),
                   pl.BlockSpec((tr, D_MODEL), lambda i: (jnp.maximum(i - n_prompt_tiles, 0), 0))],
        out_shape=[jax.ShapeDtypeStruct((n_prompt_tiles * tr, D_MODEL), F32),
                   jax.ShapeDtypeStruct(((nt - n_prompt_tiles) * tr, D_MODEL), F32)],
        scratch_shapes=[pltpu.VMEM((TOP_K, tr, D_MODEL), F32), pltpu.SemaphoreType.DMA],
        compiler_params=_cparams(("arbitrary",)),
        name="moe_combine",
    )(dest3, meta, x1_all, ys)


def _moe(x1_prompt, x1_sample, norm2_w, w_router, b_router, w_e1, b_e1, w_e2, b_e2):
    tp, ts = x1_prompt.shape[0], x1_sample.shape[0]
    n_prompt_tiles = tp // ROUTE_TILE
    x1_all = jnp.concatenate([x1_prompt, x1_sample, jnp.zeros((ROUTE_TILE - ts, D_MODEL), F32)], axis=0)
    t_all = x1_all.shape[0]
    wr_pad = jnp.pad(w_router, ((0, 0), (0, LANES - N_EXPERTS)))
    br_pad = _pad_lanes(b_router, NEG_BIG)
    h2, meta, cnt = _route(x1_all, norm2_w[None, :], wr_pad, br_pad)

    tm = EXPERT_TILE
    n_rows = t_all * TOP_K + N_EXPERTS * tm
    counts = cnt[0, :N_EXPERTS].astype(I32)
    padded = ((counts + tm - 1) // tm) * tm
    ends = jnp.cumsum(padded)
    offs = ends - padded
    idx = meta[:, META_IDX:META_IDX + TOP_K].astype(I32)
    rank = meta[:, META_RANK:META_RANK + TOP_K].astype(I32)
    dest = jnp.take(offs, idx) + rank
    dest3 = dest.reshape(t_all // ROUTE_TILE, 1, ROUTE_TILE * TOP_K)
    pad_bounds = jnp.stack([offs + counts, ends])
    n_tiles = (ends[-1] // tm).astype(I32)
    tile_ids = jnp.arange(n_rows // tm, dtype=I32)
    tile_expert = jnp.searchsorted(ends, jnp.minimum(tile_ids, n_tiles - 1) * tm, side='right').astype(I32)

    xs = _dispatch(dest3, pad_bounds, h2, n_rows)
    w1 = w_e1.astype(BF16)
    ys = _experts(tile_expert, n_tiles[None], xs, w1[:, :, 0::2], w1[:, :, 1::2],
                  b_e1[:, None, 0::2], b_e1[:, None, 1::2], w_e2.astype(BF16), b_e2[:, None, :])
    y_prompt, y_sample = _combine(dest3, meta, x1_all, ys, n_prompt_tiles)
    return y_prompt, y_sample[:ts]


def _permute_w_in(w):
    dt_lo = D_INNER + CONV_DIM
    dt_hi = dt_lo + SSM_HEADS
    pad = jnp.zeros((w.shape[0], PROJ_WIDTH - w.shape[1]), w.dtype)
    return jnp.concatenate([w[:, :dt_lo], w[:, dt_hi:], w[:, dt_lo:dt_hi], pad], axis=1)


def _prep_common(conv_w, conv_b, dt_bias, a_log, d_skip, ssm_norm_w, q_norm_w, k_norm_w,
                 lambda_q1, lambda_k1, lambda_q2, lambda_k2, attn_subln_w):
    return {
        'conv_w': conv_w, 'conv_b': conv_b[None, :],
        'dt_bias': _pad_lanes(dt_bias), 'a_log': _pad_lanes(a_log),
        'd_skip': jnp.repeat(d_skip, SSM_HEAD_DIM)[None, :], 'ssm_norm_w': ssm_norm_w[None, :],
        'q_norm_w': jnp.tile(q_norm_w, 2)[None, :], 'k_norm_w': jnp.tile(k_norm_w, 2)[None, :],
        'lambda_q1': lambda_q1[None, :], 'lambda_k1': lambda_k1[None, :],
        'lambda_q2': lambda_q2[None, :], 'lambda_k2': lambda_k2[None, :],
        'attn_subln_w': attn_subln_w[None, :],
    }


def _prompt_mixer(x_prompt, norm1_w, w_in_p, p, w_br_ssm, w_br_attn, w_o):
    batch, seq, _ = x_prompt.shape
    x2d = x_prompt.reshape(batch * seq, D_MODEL)
    tables = _rope_tables(jnp.arange(seq))
    z, xbc, q, kf, kb, vf, vb, g, dt = _inproj_prompt(
        x2d, norm1_w[None, :], w_in_p.astype(BF16), p['q_norm_w'], p['k_norm_w'], tables, seq, tm=256)
    y_ssm, h_fin, conv_fin = _ssd_prompt(xbc, z, dt, p, batch, seq)
    o = _flash_prompt(q, kb, vb, p, batch, seq, tq=min(512, seq))
    x1 = _merge(y_ssm, o, g, x2d, w_br_ssm.astype(BF16), w_br_attn.astype(BF16), w_o.astype(BF16),
                tm=256, prec=None, name="merge_prompt")
    return x1, kf, vf, h_fin, conv_fin


def kernel(x_prompt, x_sample, cache_k, cache_v, state_ssm, state_conv, page_table, norm1_w, w_in, conv_w, conv_b, dt_bias, a_log, d_skip, ssm_norm_w, q_norm_w, k_norm_w, lambda_q1, lambda_k1, lambda_q2, lambda_k2, attn_subln_w, w_br_ssm, w_br_attn, w_o, norm2_w, w_router, b_router, w_e1, b_e1, w_e2, b_e2):
    p = _prep_common(conv_w[0], conv_b[0], dt_bias[0], a_log[0], d_skip[0], ssm_norm_w[0], q_norm_w[0], k_norm_w[0],
                     lambda_q1[0], lambda_k1[0], lambda_q2[0], lambda_k2[0], attn_subln_w[0])
    w_in_p = _permute_w_in(w_in[0])
    bp, seq, _ = x_prompt.shape
    db, ts, _ = x_sample.shape
    assert ts == 1 and db <= ROUTE_TILE and (bp * seq) % ROUTE_TILE == 0
    past_len = page_table.shape[1] * PAGE_SIZE

    x1p, k_p, v_p, h_p, conv_p = _prompt_mixer(x_prompt, norm1_w[0], w_in_p, p, w_br_ssm[0], w_br_attn[0], w_o[0])
    x1s, k_s, v_s, h_s, conv_s = _sample_mixer(x_sample, state_ssm[0], state_conv[0], cache_k, cache_v, page_table,
                                               norm1_w[0], w_in_p, p, w_br_ssm[0], w_br_attn[0], w_o[0], past_len)
    y_p, y_s = _moe(x1p, x1s, norm2_w[0], w_router[0], b_router[0], w_e1[0], b_e1[0], w_e2[0], b_e2[0])

    heads = lambda a, b, t: a.reshape(1, b, t, ATTN_HEADS, ATTN_V_DIM)
    state = lambda a, b: a.reshape(1, b, SSM_HEADS, SSM_HEAD_DIM, D_STATE)
    return (y_p.reshape(bp, seq, D_MODEL), y_s.reshape(db, ts, D_MODEL),
            heads(k_p, bp, seq), heads(v_p, bp, seq), state(h_p, bp), conv_p[None],
            heads(k_s, db, ts), heads(v_s, db, ts), state(h_s, db), conv_s[None])
```
